```python
import jax, jax.numpy as jnp
from jax import lax
import numpy as np

D_MODEL = 1024
BATCH = 16
SEQ = 4096
DEPTH = 1
DEC_BATCH = 8
DEC_SEQ = 4096
PAST_LEN = 128

D_MIX = D_MODEL
A_WIDTH = D_MIX // 2
A_HEADS = 8
A_HEAD_DIM = A_WIDTH // A_HEADS
CHUNK = 128
B_WIDTH = D_MIX - A_WIDTH
B_GROUPS = 8
CONV_K = 31
CONV_PAD = CONV_K // 2
D_FF = 4 * D_MODEL
N_MOD = 6
EPS = 1e-6

kernel_name = "hybrid_sgu_conformer_conv_encoder_adaln"


def rms_norm(x, g):
    xf = x.astype(jnp.float32)
    y = xf * lax.rsqrt(jnp.mean(xf * xf, axis=-1, keepdims=True) + EPS)
    return (y * g.astype(jnp.float32)).astype(x.dtype)


def layer_norm(x, g, b):
    xf = x.astype(jnp.float32)
    mu = jnp.mean(xf, axis=-1, keepdims=True)
    var = jnp.mean(jnp.square(xf - mu), axis=-1, keepdims=True)
    y = (xf - mu) * lax.rsqrt(var + EPS)
    return (y * g.astype(jnp.float32) + b.astype(jnp.float32)).astype(x.dtype)


def modulate(h, shift, scale):
    return h * (1 + scale[:, None, :]) + shift[:, None, :]


def encoder_layer(x, c, ada_w, ada_b, norm1_g, w_in, sgu_norm_g, w_s, b_s,
                  conv_w, conv_b, conv_ln_g, conv_ln_b, w_out, norm2_g, w_ff1, w_ff2):
    bsz, seq, _ = x.shape
    mod = jax.nn.silu(c) @ ada_w + ada_b
    sh1, sc1, g1, sh2, sc2, g2 = jnp.split(mod, N_MOD, axis=-1)

    h = modulate(rms_norm(x, norm1_g), sh1, sc1)
    z = h @ w_in
    u, v, ga, gb = jnp.split(z, [A_WIDTH, 2 * A_WIDTH, 2 * A_WIDTH + B_WIDTH], axis=-1)

    v = rms_norm(v, sgu_norm_g)
    vc = v.reshape(bsz, seq // CHUNK, CHUNK, A_HEADS, A_HEAD_DIM)
    mixed = jnp.einsum('hpq,bnqhd->bnphd', w_s, vc) + b_s.T[None, None, :, :, None]
    y_a = u * mixed.reshape(bsz, seq, A_WIDTH)

    glu = ga * jax.nn.sigmoid(gb)
    conv = lax.conv_general_dilated(
        glu, conv_w[:, None, :], window_strides=(1,), padding=[(CONV_PAD, CONV_PAD)],
        dimension_numbers=('NWC', 'WIO', 'NWC'), feature_group_count=B_WIDTH)
    y_b = jax.nn.silu(layer_norm(conv + conv_b, conv_ln_g, conv_ln_b))

    y = jnp.concatenate([y_a, y_b], axis=-1) @ w_out
    x = x + g1[:, None, :] * y

    h2 = modulate(rms_norm(x, norm2_g), sh2, sc2)
    f = jnp.square(jax.nn.relu(h2 @ w_ff1)) @ w_ff2
    x = x + g2[:, None, :] * f
    return x


def setup_inputs(seed: int = 0) -> dict:
    key = jax.random.key(seed)
    ks = jax.random.split(key, 24)
    nrm = lambda k, shape, s: jax.random.normal(k, shape, jnp.float32) * s
    L = DEPTH
    return {
        "x_prompt": nrm(ks[0], (BATCH, SEQ, D_MODEL), 1.0),
        "x_sample": nrm(ks[1], (DEC_BATCH, DEC_SEQ, D_MODEL), 1.0),
        "c_prompt": nrm(ks[2], (BATCH, D_MODEL), 1.0),
        "c_sample": nrm(ks[3], (DEC_BATCH, D_MODEL), 1.0),
        "ada_w": nrm(ks[4], (L, D_MODEL, N_MOD * D_MODEL), D_MODEL ** -0.5),
        "ada_b": nrm(ks[5], (L, N_MOD * D_MODEL), 0.02),
        "norm1_g": 1.0 + nrm(ks[6], (L, D_MODEL), 0.02),
        "w_in": nrm(ks[7], (L, D_MODEL, 2 * A_WIDTH + 2 * B_WIDTH), D_MODEL ** -0.5),
        "sgu_norm_g": 1.0 + nrm(ks[8], (L, A_WIDTH), 0.02),
        "w_s": nrm(ks[9], (L, A_HEADS, CHUNK, CHUNK), CHUNK ** -0.5),
        "b_s": 1.0 + nrm(ks[10], (L, A_HEADS, CHUNK), 0.02),
        "conv_w": nrm(ks[11], (L, CONV_K, B_WIDTH), CONV_K ** -0.5),
        "conv_b": nrm(ks[12], (L, B_WIDTH), 0.02),
        "conv_ln_g": 1.0 + nrm(ks[13], (L, B_WIDTH), 0.02),
        "conv_ln_b": nrm(ks[14], (L, B_WIDTH), 0.02),
        "w_out": nrm(ks[15], (L, D_MIX, D_MODEL), D_MIX ** -0.5),
        "norm2_g": 1.0 + nrm(ks[16], (L, D_MODEL), 0.02),
        "w_ff1": nrm(ks[17], (L, D_MODEL, D_FF), D_MODEL ** -0.5),
        "w_ff2": nrm(ks[18], (L, D_FF, D_MODEL), D_FF ** -0.5),
        "final_g": 1.0 + nrm(ks[19], (D_MODEL,), 0.02),
    }


def run_trunk(x, c, ada_w, ada_b, norm1_g, w_in, sgu_norm_g, w_s, b_s, conv_w, conv_b,
              conv_ln_g, conv_ln_b, w_out, norm2_g, w_ff1, w_ff2, final_g):
    for l in range(DEPTH):
        x = encoder_layer(x, c, ada_w[l], ada_b[l], norm1_g[l], w_in[l], sgu_norm_g[l],
                          w_s[l], b_s[l], conv_w[l], conv_b[l], conv_ln_g[l], conv_ln_b[l],
                          w_out[l], norm2_g[l], w_ff1[l], w_ff2[l])
    return rms_norm(x, final_g)


def reference(x_prompt, x_sample, c_prompt, c_sample, ada_w, ada_b, norm1_g, w_in,
              sgu_norm_g, w_s, b_s, conv_w, conv_b, conv_ln_g, conv_ln_b, w_out,
              norm2_g, w_ff1, w_ff2, final_g):
    y_prompt = run_trunk(x_prompt, c_prompt, ada_w, ada_b, norm1_g, w_in, sgu_norm_g, w_s,
                         b_s, conv_w, conv_b, conv_ln_g, conv_ln_b, w_out, norm2_g,
                         w_ff1, w_ff2, final_g)
    y_sample = run_trunk(x_sample, c_sample, ada_w, ada_b, norm1_g, w_in, sgu_norm_g, w_s,
                         b_s, conv_w, conv_b, conv_ln_g, conv_ln_b, w_out, norm2_g,
                         w_ff1, w_ff2, final_g)
    return (y_prompt, y_sample)
```

```python
import functools

import jax
import jax.numpy as jnp
from jax import lax
from jax.experimental import pallas as pl
from jax.experimental.pallas import tpu as pltpu

D_MODEL = 1024
A_WIDTH = 512
A_HEADS = 8
A_HEAD_DIM = A_WIDTH // A_HEADS
CHUNK = 128
B_WIDTH = 512
CONV_K = 31
CONV_PAD = CONV_K // 2
D_FF = 4 * D_MODEL
N_MOD = 6
EPS = 1e-6

LANES = 128
HALO = 16
SEQ_TILE = 512
FFN_TILE = 512
FF_CHUNK = 1024
CONV_ROWS = 64
VMEM_LIMIT_BYTES = 56 * 1024 * 1024

BF16 = jnp.bfloat16
F32 = jnp.float32


def _dot(a, b):
    return jnp.dot(a, b, preferred_element_type=F32)


def _sigmoid(x):
    return 1.0 / (1.0 + jnp.exp(-x))


def _mod_kernel(c_ref, w_ref, b_ref, o_ref):
    c = c_ref[...]
    s = (c * _sigmoid(c)).astype(BF16)
    o_ref[...] = _dot(s, w_ref[...]) + b_ref[...]


def _modulation(c_all, ada_w_bf, ada_b):
    nb = c_all.shape[0]
    return pl.pallas_call(
        _mod_kernel,
        out_shape=jax.ShapeDtypeStruct((nb, N_MOD * D_MODEL), F32),
        grid=(N_MOD,),
        in_specs=[
            pl.BlockSpec((nb, D_MODEL), lambda j: (0, 0)),
            pl.BlockSpec((D_MODEL, D_MODEL), lambda j: (0, j)),
            pl.BlockSpec((1, D_MODEL), lambda j: (0, j)),
        ],
        out_specs=pl.BlockSpec((nb, D_MODEL), lambda j: (0, j)),
        compiler_params=pltpu.CompilerParams(dimension_semantics=("arbitrary",)),
        name="adaln_modulation",
    )(c_all, ada_w_bf, ada_b)


def _mixer_kernel(x_ref, xp_ref, xn_ref, mod_ref, n1g_ref, win_ref, sgug_ref, ws_ref, bsf_ref,
                  cw_ref, cb_ref, lng_ref, lnb_ref, wout_ref, o_ref, vn_ref, glu_ref, ycat_ref):
    t = pl.program_id(1)
    n_t = pl.num_programs(1)
    tile = x_ref.shape[1]

    mod = mod_ref[0]
    sh1, sc1, g1 = mod[0:1], mod[1:2], mod[2:3]
    a1 = n1g_ref[...] * (1.0 + sc1)

    def norm_mod(xv):
        ms = jnp.mean(xv * xv, axis=-1, keepdims=True)
        return (xv * lax.rsqrt(ms + EPS) * a1 + sh1).astype(BF16)

    x = x_ref[0]
    h = norm_mod(x)

    v = _dot(h, win_ref[:, A_WIDTH:2 * A_WIDTH])
    vms = jnp.mean(v * v, axis=-1, keepdims=True)
    vn_ref[...] = (v * lax.rsqrt(vms + EPS) * sgug_ref[...]).astype(BF16)

    ga = _dot(h, win_ref[:, 2 * A_WIDTH:2 * A_WIDTH + B_WIDTH])
    gb = _dot(h, win_ref[:, 2 * A_WIDTH + B_WIDTH:])
    glu_ref[HALO:HALO + tile, :] = ga * _sigmoid(gb)
    hh = norm_mod(jnp.concatenate([xp_ref[0], xn_ref[0]], axis=0))
    gah = _dot(hh, win_ref[:, 2 * A_WIDTH:2 * A_WIDTH + B_WIDTH])
    gbh = _dot(hh, win_ref[:, 2 * A_WIDTH + B_WIDTH:])
    gluh = gah * _sigmoid(gbh)
    glu_ref[0:HALO, :] = jnp.where(t > 0, gluh[0:HALO], 0.0)
    glu_ref[HALO + tile:, :] = jnp.where(t < n_t - 1, gluh[HALO:], 0.0)

    u = _dot(h, win_ref[:, 0:A_WIDTH])
    lane = lax.broadcasted_iota(jnp.int32, (CHUNK, LANES), 1)
    first_head = lane < A_HEAD_DIM
    for n in range(tile // CHUNK):
        rows = slice(n * CHUNK, (n + 1) * CHUNK)
        for j in range(A_HEADS // 2):
            cols = slice(j * LANES, (j + 1) * LANES)
            vg = vn_ref[rows, cols]
            mixed = jnp.where(first_head, _dot(ws_ref[2 * j], vg), _dot(ws_ref[2 * j + 1], vg))
            ycat_ref[rows, cols] = (u[rows, cols] * (mixed + bsf_ref[:, cols])).astype(BF16)

    for r0 in range(0, tile, CONV_ROWS):
        acc = jnp.zeros((CONV_ROWS, B_WIDTH), F32) + cb_ref[...]
        for k in range(CONV_K):
            off = r0 + HALO - CONV_PAD + k
            acc = acc + glu_ref[off:off + CONV_ROWS, :] * cw_ref[k:k + 1, :]
        mu = jnp.mean(acc, axis=-1, keepdims=True)
        d = acc - mu
        var = jnp.mean(d * d, axis=-1, keepdims=True)
        ln = d * lax.rsqrt(var + EPS) * lng_ref[...] + lnb_ref[...]
        ycat_ref[r0:r0 + CONV_ROWS, A_WIDTH:] = (ln * _sigmoid(ln)).astype(BF16)

    y = _dot(ycat_ref[...], wout_ref[...])
    o_ref[0] = x + g1 * y


def _mixer(x, mod3, n1g, win_bf, sgug, ws_bf, bsf, cw, cb, lng, lnb, wout_bf):
    nb, seq, d = x.shape
    tile = SEQ_TILE
    n_t = seq // tile
    hpt = tile // HALO
    n_hb = seq // HALO

    const2 = lambda b, t: (0, 0)
    const3 = lambda b, t: (0, 0, 0)
    return pl.pallas_call(
        _mixer_kernel,
        out_shape=jax.ShapeDtypeStruct((nb, seq, d), F32),
        grid=(nb, n_t),
        in_specs=[
            pl.BlockSpec((1, tile, d), lambda b, t: (b, t, 0)),
            pl.BlockSpec((1, HALO, d), lambda b, t: (b, jnp.maximum(t * hpt - 1, 0), 0)),
            pl.BlockSpec((1, HALO, d), lambda b, t: (b, jnp.minimum((t + 1) * hpt, n_hb - 1), 0)),
            pl.BlockSpec((1, N_MOD, d), lambda b, t: (b, 0, 0)),
            pl.BlockSpec((1, d), const2),
            pl.BlockSpec(win_bf.shape, const2),
            pl.BlockSpec((1, A_WIDTH), const2),
            pl.BlockSpec(ws_bf.shape, const3),
            pl.BlockSpec(bsf.shape, const2),
            pl.BlockSpec(cw.shape, const2),
            pl.BlockSpec((1, B_WIDTH), const2),
            pl.BlockSpec((1, B_WIDTH), const2),
            pl.BlockSpec((1, B_WIDTH), const2),
            pl.BlockSpec(wout_bf.shape, const2),
        ],
        out_specs=pl.BlockSpec((1, tile, d), lambda b, t: (b, t, 0)),
        scratch_shapes=[
            pltpu.VMEM((tile, A_WIDTH), BF16),
            pltpu.VMEM((tile + 2 * HALO, B_WIDTH), F32),
            pltpu.VMEM((tile, A_WIDTH + B_WIDTH), BF16),
        ],
        compiler_params=pltpu.CompilerParams(
            dimension_semantics=("parallel", "arbitrary"),
            vmem_limit_bytes=VMEM_LIMIT_BYTES,
        ),
        name="token_mixer",
    )(x, x, x, mod3, n1g, win_bf, sgug, ws_bf, bsf, cw, cb, lng, lnb, wout_bf)


def _ffn_kernel(x_ref, mod_ref, n2g_ref, w1_ref, w2_ref, fg_ref, o_ref):
    mod = mod_ref[0]
    sh2, sc2, g2 = mod[3:4], mod[4:5], mod[5:6]
    a2 = n2g_ref[...] * (1.0 + sc2)

    x = x_ref[0]
    ms = jnp.mean(x * x, axis=-1, keepdims=True)
    h2 = (x * lax.rsqrt(ms + EPS) * a2 + sh2).astype(BF16)

    f = None
    for c in range(0, D_FF, FF_CHUNK):
        hid = jnp.maximum(_dot(h2, w1_ref[:, c:c + FF_CHUNK]), 0.0)
        part = _dot((hid * hid).astype(BF16), w2_ref[c:c + FF_CHUNK, :])
        f = part if f is None else f + part

    x2 = x + g2 * f
    ms2 = jnp.mean(x2 * x2, axis=-1, keepdims=True)
    o_ref[0] = x2 * lax.rsqrt(ms2 + EPS) * fg_ref[...]


def _ffn(x, mod3, n2g, w1_bf, w2_bf, fg):
    nb, seq, d = x.shape
    tile = FFN_TILE
    const2 = lambda b, t: (0, 0)
    return pl.pallas_call(
        _ffn_kernel,
        out_shape=jax.ShapeDtypeStruct((nb, seq, d), F32),
        grid=(nb, seq // tile),
        in_specs=[
            pl.BlockSpec((1, tile, d), lambda b, t: (b, t, 0)),
            pl.BlockSpec((1, N_MOD, d), lambda b, t: (b, 0, 0)),
            pl.BlockSpec((1, d), const2),
            pl.BlockSpec(w1_bf.shape, const2),
            pl.BlockSpec(w2_bf.shape, const2),
            pl.BlockSpec((1, d), const2),
        ],
        out_specs=pl.BlockSpec((1, tile, d), lambda b, t: (b, t, 0)),
        compiler_params=pltpu.CompilerParams(
            dimension_semantics=("parallel", "arbitrary"),
            vmem_limit_bytes=VMEM_LIMIT_BYTES,
        ),
        name="ffn_final_norm",
    )(x, mod3, n2g, w1_bf, w2_bf, fg)


def kernel(x_prompt, x_sample, c_prompt, c_sample, ada_w, ada_b, norm1_g, w_in, sgu_norm_g, w_s, b_s,
           conv_w, conv_b, conv_ln_g, conv_ln_b, w_out, norm2_g, w_ff1, w_ff2, final_g):
    assert ada_w.shape[0] == 1, "single-layer trunk"
    assert x_prompt.shape[1] % SEQ_TILE == 0 and x_sample.shape[1] % SEQ_TILE == 0
    n_prompt = x_prompt.shape[0]

    c_all = jnp.concatenate([c_prompt, c_sample], axis=0)
    mod = _modulation(c_all, ada_w[0].astype(BF16), ada_b)
    mod3 = mod.reshape(c_all.shape[0], N_MOD, D_MODEL)

    win_bf = w_in[0].astype(BF16)
    ws_bf = w_s[0].astype(BF16)
    wout_bf = w_out[0].astype(BF16)
    w1_bf = w_ff1[0].astype(BF16)
    w2_bf = w_ff2[0].astype(BF16)
    bsf = jnp.repeat(b_s[0].T, A_HEAD_DIM, axis=1)
    fg = final_g.reshape(1, D_MODEL)

    def trunk(x, m3):
        x1 = _mixer(x, m3, norm1_g, win_bf, sgu_norm_g, ws_bf, bsf, conv_w[0], conv_b,
                    conv_ln_g, conv_ln_b, wout_bf)
        return _ffn(x1, m3, norm2_g, w1_bf, w2_bf, fg)

    return trunk(x_prompt, mod3[:n_prompt]), trunk(x_sample, mod3[n_prompt:])
```

```python
import jax
import jax.numpy as jnp
from jax import lax
from jax.experimental import pallas as pl
from jax.experimental.pallas import tpu as pltpu

D_MODEL = 1024
A_WIDTH = 512
A_HEADS = 8
A_HEAD_DIM = A_WIDTH // A_HEADS
CHUNK = 128
B_WIDTH = 512
CONV_K = 31
CONV_PAD = CONV_K // 2
D_FF = 4 * D_MODEL
N_MOD = 6
EPS = 1e-6

LANES = 128
SUBLANES = 8
HALO = 16
SEQ_TILE = 512
SUB_BLOCK = 256
FF_CHUNK = 1024
CONV_ROWS = 64
CONV_WIN = CONV_ROWS + 2 * HALO
VMEM_LIMIT_BYTES = 56 * 1024 * 1024

BF16 = jnp.bfloat16
F32 = jnp.float32


def _dot(a, b):
    return jnp.dot(a, b, preferred_element_type=F32)


def _sigmoid(x):
    return 1.0 / (1.0 + jnp.exp(-x))


def _mod_kernel(c_ref, w_ref, b_ref, o_ref):
    c = c_ref[...]
    s = (c * _sigmoid(c)).astype(BF16)
    o_ref[...] = _dot(s, w_ref[...]) + b_ref[...]


def _modulation(c_all, ada_w_bf, ada_b):
    nb = c_all.shape[0]
    return pl.pallas_call(
        _mod_kernel,
        out_shape=jax.ShapeDtypeStruct((nb, N_MOD * D_MODEL), F32),
        grid=(N_MOD,),
        in_specs=[
            pl.BlockSpec((nb, D_MODEL), lambda j: (0, 0)),
            pl.BlockSpec((D_MODEL, D_MODEL), lambda j: (0, j)),
            pl.BlockSpec((1, D_MODEL), lambda j: (0, j)),
        ],
        out_specs=pl.BlockSpec((nb, D_MODEL), lambda j: (0, j)),
        compiler_params=pltpu.CompilerParams(dimension_semantics=("arbitrary",)),
        name="adaln_modulation",
    )(c_all, ada_w_bf, ada_b)


def _conv_block(glu_ref, r, r0, cw_ref, cb_ref):
    win = glu_ref[r, r0:r0 + CONV_WIN, :]
    acc = jnp.zeros((CONV_ROWS, B_WIDTH), F32) + cb_ref[...]
    for b in range(SUBLANES):
        wb = win if b == 0 else pltpu.roll(win, CONV_WIN - b, axis=0)
        for a in range(CONV_K // SUBLANES + 1):
            k = SUBLANES * a + b - 1
            if 0 <= k < CONV_K:
                acc = acc + wb[SUBLANES * a:SUBLANES * a + CONV_ROWS, :] * cw_ref[k:k + 1, :]
    return acc


def _layer_kernel(x_ref, xp_ref, xn_ref, mod_ref, n1g_ref, win_ref, sgug_ref, wsp_ref, bsf_ref,
                  cw_ref, cb_ref, lng_ref, lnb_ref, wout_ref, n2g_ref, w1_ref, w2_ref, fg_ref,
                  o_ref, hext_ref, glu_ref, ycat_ref):
    t = pl.program_id(1)
    n_t = pl.num_programs(1)
    tile = x_ref.shape[1]
    n_sub = tile // SUB_BLOCK

    mod = mod_ref[0]
    sh1, sc1, g1 = mod[0:1], mod[1:2], mod[2:3]
    sh2, sc2, g2 = mod[3:4], mod[4:5], mod[5:6]
    a1 = n1g_ref[...] * (1.0 + sc1)
    a2 = n2g_ref[...] * (1.0 + sc2)

    def norm_mod(xv, a, sh):
        ms = jnp.mean(xv * xv, axis=-1, keepdims=True)
        return (xv * lax.rsqrt(ms + EPS) * a + sh).astype(BF16)

    lane = lax.broadcasted_iota(jnp.int32, (CHUNK, LANES), 1)
    first_head = lane < A_HEAD_DIM

    for r in range(n_sub):
        lo = r * SUB_BLOCK
        x = x_ref[0, lo:lo + SUB_BLOCK, :]
        prev = xp_ref[0] if r == 0 else x_ref[0, lo - HALO:lo, :]
        nxt = xn_ref[0] if r == n_sub - 1 else x_ref[0, lo + SUB_BLOCK:lo + SUB_BLOCK + HALO, :]

        hext_ref[r, 0:HALO, :] = norm_mod(prev, a1, sh1)
        hext_ref[r, HALO:HALO + SUB_BLOCK, :] = norm_mod(x, a1, sh1)
        hext_ref[r, HALO + SUB_BLOCK:, :] = norm_mod(nxt, a1, sh1)

        gg = _dot(hext_ref[r], win_ref[:, 2 * A_WIDTH:])
        glu = gg[:, :B_WIDTH] * _sigmoid(gg[:, B_WIDTH:])
        head, body, tail = glu[0:HALO], glu[HALO:HALO + SUB_BLOCK], glu[HALO + SUB_BLOCK:]
        if r == 0:
            head = jnp.where(t > 0, head, 0.0)
        if r == n_sub - 1:
            tail = jnp.where(t < n_t - 1, tail, 0.0)
        glu_ref[r, 0:HALO, :] = head
        glu_ref[r, HALO:HALO + SUB_BLOCK, :] = body
        glu_ref[r, HALO + SUB_BLOCK:, :] = tail

        uv = _dot(hext_ref[r, HALO:HALO + SUB_BLOCK, :], win_ref[:, :2 * A_WIDTH])
        u, v = uv[:, :A_WIDTH], uv[:, A_WIDTH:]
        vms = jnp.mean(v * v, axis=-1, keepdims=True)
        vn = (v * lax.rsqrt(vms + EPS) * sgug_ref[...]).astype(BF16)

        for n in range(SUB_BLOCK // CHUNK):
            rows = slice(n * CHUNK, (n + 1) * CHUNK)
            for j in range(A_HEADS // 2):
                cols = slice(j * LANES, (j + 1) * LANES)
                vg = vn[rows, cols]
                zero = jnp.zeros_like(vg)
                rhs = jnp.concatenate([jnp.where(first_head, vg, zero), jnp.where(first_head, zero, vg)], axis=0)
                mixed = _dot(wsp_ref[j], rhs)
                ycat_ref[r, rows, cols] = (u[rows, cols] * (mixed + bsf_ref[:, cols])).astype(BF16)

        for r0 in range(0, SUB_BLOCK, CONV_ROWS):
            acc = _conv_block(glu_ref, r, r0, cw_ref, cb_ref)
            mu = jnp.mean(acc, axis=-1, keepdims=True)
            d = acc - mu
            var = jnp.mean(d * d, axis=-1, keepdims=True)
            ln = d * lax.rsqrt(var + EPS) * lng_ref[...] + lnb_ref[...]
            ycat_ref[r, r0:r0 + CONV_ROWS, A_WIDTH:] = (ln * _sigmoid(ln)).astype(BF16)

        x1 = x + g1 * _dot(ycat_ref[r], wout_ref[...])

        h2 = norm_mod(x1, a2, sh2)
        f = None
        for c in range(0, D_FF, FF_CHUNK):
            hid = jnp.maximum(_dot(h2, w1_ref[:, c:c + FF_CHUNK]), 0.0)
            part = _dot((hid * hid).astype(BF16), w2_ref[c:c + FF_CHUNK, :])
            f = part if f is None else f + part
        x2 = x1 + g2 * f
        ms2 = jnp.mean(x2 * x2, axis=-1, keepdims=True)
        o_ref[0, lo:lo + SUB_BLOCK, :] = x2 * lax.rsqrt(ms2 + EPS) * fg_ref[...]


def _resident(shape):
    zeros = (0,) * len(shape)
    return pl.BlockSpec(shape, lambda b, t: zeros, pipeline_mode=pl.Buffered(1))


def _layer(x, mod3, n1g, win_bf, sgug, wsp_bf, bsf, cw, cb, lng, lnb, wout_bf, n2g, w1_bf, w2_bf, fg):
    nb, seq, d = x.shape
    tile = SEQ_TILE
    n_sub = tile // SUB_BLOCK
    hpt = tile // HALO
    n_hb = seq // HALO

    return pl.pallas_call(
        _layer_kernel,
        out_shape=jax.ShapeDtypeStruct((nb, seq, d), F32),
        grid=(nb, seq // tile),
        in_specs=[
            pl.BlockSpec((1, tile, d), lambda b, t: (b, t, 0)),
            pl.BlockSpec((1, HALO, d), lambda b, t: (b, jnp.maximum(t * hpt - 1, 0), 0)),
            pl.BlockSpec((1, HALO, d), lambda b, t: (b, jnp.minimum((t + 1) * hpt, n_hb - 1), 0)),
            pl.BlockSpec((1, N_MOD, d), lambda b, t: (b, 0, 0)),
            _resident((1, d)),
            _resident(win_bf.shape),
            _resident((1, A_WIDTH)),
            _resident(wsp_bf.shape),
            _resident(bsf.shape),
            _resident(cw.shape),
            _resident((1, B_WIDTH)),
            _resident((1, B_WIDTH)),
            _resident((1, B_WIDTH)),
            _resident(wout_bf.shape),
            _resident((1, d)),
            _resident(w1_bf.shape),
            _resident(w2_bf.shape),
            _resident((1, d)),
        ],
        out_specs=pl.BlockSpec((1, tile, d), lambda b, t: (b, t, 0)),
        scratch_shapes=[
            pltpu.VMEM((n_sub, SUB_BLOCK + 2 * HALO, d), BF16),
            pltpu.VMEM((n_sub, SUB_BLOCK + 2 * HALO, B_WIDTH), F32),
            pltpu.VMEM((n_sub, SUB_BLOCK, A_WIDTH + B_WIDTH), BF16),
        ],
        compiler_params=pltpu.CompilerParams(
            dimension_semantics=("parallel", "arbitrary"),
            vmem_limit_bytes=VMEM_LIMIT_BYTES,
        ),
        name="encoder_layer",
    )(x, x, x, mod3, n1g, win_bf, sgug, wsp_bf, bsf, cw, cb, lng, lnb, wout_bf, n2g, w1_bf, w2_bf, fg)


def kernel(x_prompt, x_sample, c_prompt, c_sample, ada_w, ada_b, norm1_g, w_in, sgu_norm_g, w_s, b_s,
           conv_w, conv_b, conv_ln_g, conv_ln_b, w_out, norm2_g, w_ff1, w_ff2, final_g):
    assert ada_w.shape[0] == 1, "single-layer trunk"
    assert x_prompt.shape[1] % SEQ_TILE == 0 and x_sample.shape[1] % SEQ_TILE == 0
    assert SEQ_TILE % SUB_BLOCK == 0 and SUB_BLOCK % CHUNK == 0 and SUB_BLOCK % CONV_ROWS == 0
    n_prompt = x_prompt.shape[0]

    c_all = jnp.concatenate([c_prompt, c_sample], axis=0)
    mod = _modulation(c_all, ada_w[0].astype(BF16), ada_b)
    mod3 = mod.reshape(c_all.shape[0], N_MOD, D_MODEL)

    win_bf = w_in[0].astype(BF16)
    wout_bf = w_out[0].astype(BF16)
    w1_bf = w_ff1[0].astype(BF16)
    w2_bf = w_ff2[0].astype(BF16)
    wsp_bf = (w_s[0].reshape(A_HEADS // 2, 2, CHUNK, CHUNK).transpose(0, 2, 1, 3)
              .reshape(A_HEADS // 2, CHUNK, 2 * CHUNK).astype(BF16))
    bsf = jnp.repeat(b_s[0].T, A_HEAD_DIM, axis=1)
    fg = final_g.reshape(1, D_MODEL)

    def trunk(x, m3):
        return _layer(x, m3, norm1_g, win_bf, sgu_norm_g, wsp_bf, bsf, conv_w[0], conv_b,
                      conv_ln_g, conv_ln_b, wout_bf, norm2_g, w1_bf, w2_bf, fg)

    return trunk(x_prompt, mod3[:n_prompt]), trunk(x_sample, mod3[n_prompt:])
```
